```python
import jax, jax.numpy as jnp
from jax import lax
import numpy as np

D_MODEL = 2048
BATCH = 2
SEQ = 4096
DEPTH = 1

EPS = 1e-6
M_WIDTH = D_MODEL // 2
A_WIDTH = D_MODEL - M_WIDTH
MIX_WIDTH = M_WIDTH + A_WIDTH
M_HEADS = 4
M_HEAD_DIM = M_WIDTH // M_HEADS
CONV_W = 4
CHUNK = 64
A_HEADS = 8
A_HEAD_DIM = A_WIDTH // A_HEADS
DILATED = ((128, 1), (512, 4), (2048, 16))
SEG_PAD = 2048
ROPE_THETA = 10000.0
IN_COLS = 3 * M_WIDTH + 2 * M_HEADS + 3 * A_WIDTH
IN_SPLITS = (M_WIDTH, 2 * M_WIDTH, 3 * M_WIDTH, 3 * M_WIDTH + M_HEADS, 3 * M_WIDTH + 2 * M_HEADS,
             3 * M_WIDTH + 2 * M_HEADS + A_WIDTH, 3 * M_WIDTH + 2 * M_HEADS + 2 * A_WIDTH)
N_MEM = 256
X_HEADS = 4
X_HEAD_DIM = D_MODEL // X_HEADS
PEER_HEADS = 8
N_KEYS = 128
N_EXPERTS = N_KEYS * N_KEYS
PK_DIM = 256
PK_TOPK = 16
PEER_BLOCK = 128

kernel_name = "hybrid_mlstm_dilated_peer_block"


def rmsnorm(x, g):
    xf = x.astype(jnp.float32)
    y = xf * lax.rsqrt(jnp.mean(xf * xf, axis=-1, keepdims=True) + EPS)
    return (y * g.astype(jnp.float32)).astype(x.dtype)


def heads(t, n_heads):
    b, s, _ = t.shape
    return t.reshape(b, s, n_heads, -1).transpose(0, 2, 1, 3)


def rope(x, pos):
    half = x.shape[-1] // 2
    inv = ROPE_THETA ** (-jnp.arange(half, dtype=jnp.float32) / half)
    ang = pos.astype(jnp.float32)[:, None] * inv[None, :]
    cos, sin = jnp.cos(ang), jnp.sin(ang)
    x1, x2 = x[..., :half].astype(jnp.float32), x[..., half:].astype(jnp.float32)
    out = jnp.concatenate([x1 * cos - x2 * sin, x2 * cos + x1 * sin], axis=-1)
    return out.astype(x.dtype)


def causal_conv(x, w, b):
    y = lax.conv_general_dilated(x, w[:, None, :], window_strides=(1,), padding=[(CONV_W - 1, 0)],
                                 dimension_numbers=('NWC', 'WIO', 'NWC'), feature_group_count=x.shape[-1])
    return y + b


def mlstm_chunkwise(q, k, v, i_pre, f_pre):
    out_dtype = q.dtype
    B, H, S, Dh = q.shape
    nc = S // CHUNK
    q = q.astype(jnp.float32).reshape(B, H, nc, CHUNK, Dh)
    k = k.astype(jnp.float32).reshape(B, H, nc, CHUNK, Dh) * (Dh ** -0.5)
    v = v.astype(jnp.float32).reshape(B, H, nc, CHUNK, Dh)
    logi = i_pre.astype(jnp.float32).reshape(B, H, nc, CHUNK)
    logf = jax.nn.log_sigmoid(f_pre.astype(jnp.float32)).reshape(B, H, nc, CHUNK)
    b = jnp.cumsum(logf, axis=-1)
    g = b[..., -1]
    a = logi + g[..., None] - b

    def step(carry, inp):
        C, n, m = carry
        k_c, v_c, a_c, g_c = inp
        m_new = jnp.maximum(g_c + m, jnp.max(a_c, axis=-1))
        decay = jnp.exp(g_c + m - m_new)
        w = jnp.exp(a_c - m_new[..., None])
        C_new = decay[..., None, None] * C + jnp.einsum('bhl,bhld,bhle->bhde', w, k_c, v_c)
        n_new = decay[..., None] * n + jnp.einsum('bhl,bhld->bhd', w, k_c)
        return (C_new, n_new, m_new), (C, n, m)

    init = (jnp.zeros((B, H, Dh, Dh), jnp.float32), jnp.zeros((B, H, Dh), jnp.float32),
            jnp.zeros((B, H), jnp.float32))
    xs = (jnp.moveaxis(k, 2, 0), jnp.moveaxis(v, 2, 0), jnp.moveaxis(a, 2, 0), jnp.moveaxis(g, 2, 0))
    _, (C_prev, n_prev, m_prev) = lax.scan(step, init, xs)
    C_prev = jnp.moveaxis(C_prev, 0, 2)
    n_prev = jnp.moveaxis(n_prev, 0, 2)
    m_prev = jnp.moveaxis(m_prev, 0, 2)

    causal = jnp.tril(jnp.ones((CHUNK, CHUNK), dtype=bool))
    D = jnp.where(causal, b[..., :, None] - b[..., None, :] + logi[..., None, :], -jnp.inf)
    m_inter = b + m_prev[..., None]
    m_t = jnp.maximum(m_inter, jnp.max(D, axis=-1))
    s = jnp.einsum('bhcld,bhcsd->bhcls', q, k) * jnp.exp(D - m_t[..., None])
    inter = jnp.exp(m_inter - m_t)
    num = jnp.einsum('bhcls,bhcse->bhcle', s, v) + inter[..., None] * jnp.einsum('bhcld,bhcde->bhcle', q, C_prev)
    den = jnp.sum(s, axis=-1) + inter * jnp.einsum('bhcld,bhcd->bhcl', q, n_prev)
    h = num / jnp.maximum(jnp.abs(den), jnp.exp(-m_t))[..., None]
    return h.reshape(B, H, S, Dh).astype(out_dtype)


def dilated_branch(q, k, v, window, dil):
    B, H, Sp, Dh = q.shape
    blk = window // dil
    M = Sp // dil
    nb = M // blk

    def to_stream(t):
        return t.reshape(B, H, M, dil, Dh).transpose(0, 1, 3, 2, 4).reshape(B, H, dil, nb, blk, Dh)

    def from_stream(t):
        rest = t.shape[5:]
        return t.reshape((B, H, dil, M) + rest).swapaxes(2, 3).reshape((B, H, Sp) + rest)

    def with_prev(t):
        prev = jnp.pad(t, ((0, 0), (0, 0), (0, 0), (1, 0), (0, 0), (0, 0)))[:, :, :, :-1]
        return jnp.concatenate([prev, t], axis=-2)

    qs = to_stream(q)
    kw = with_prev(to_stream(k))
    vw = with_prev(to_stream(v))
    scores = jnp.einsum('bhrnqd,bhrnkd->bhrnqk', qs, kw).astype(jnp.float32) * (Dh ** -0.5)
    qi = jnp.arange(blk)[:, None]
    kj = jnp.arange(2 * blk)[None, :]
    dist = qi + blk - kj
    band = (dist >= 0) & (dist <= blk)
    valid = band[None] & ((jnp.arange(nb)[:, None, None] > 0) | (kj >= blk)[None])
    scores = jnp.where(valid, scores, -jnp.inf)
    m = jnp.max(scores, axis=-1)
    p = jnp.exp(scores - m[..., None])
    s = jnp.sum(p, axis=-1)
    o = jnp.einsum('bhrnqk,bhrnkd->bhrnqd', p.astype(v.dtype), vw).astype(jnp.float32) / s[..., None]
    return from_stream(o), from_stream(m), from_stream(s)


def dilated_mixture(q, k, v):
    B, H, S, Dh = q.shape
    Sp = -(-S // SEG_PAD) * SEG_PAD
    pad = ((0, 0), (0, 0), (0, Sp - S), (0, 0))
    q, k, v = jnp.pad(q, pad), jnp.pad(k, pad), jnp.pad(v, pad)
    outs, maxes, sums = [], [], []
    for window, dil in DILATED:
        o, m, s = dilated_branch(q, k, v, window, dil)
        outs.append(o)
        maxes.append(m)
        sums.append(s)
    o_all, m_all, s_all = jnp.stack(outs), jnp.stack(maxes), jnp.stack(sums)
    w = s_all * jnp.exp(m_all - jnp.max(m_all, axis=0, keepdims=True))
    out = jnp.sum(w[..., None] * o_all, axis=0) / jnp.sum(w, axis=0)[..., None]
    return out[:, :, :S].astype(q.dtype)


def memory_cross_attention(h, mn, w_xq, w_xk, w_xv, w_xo):
    B, S, D = h.shape
    q = (h @ w_xq).reshape(B, S, X_HEADS, X_HEAD_DIM)
    km = (mn @ w_xk).reshape(B, N_MEM, X_HEADS, X_HEAD_DIM)
    vm = (mn @ w_xv).reshape(B, N_MEM, X_HEADS, X_HEAD_DIM)
    sc = jnp.einsum('bshd,bmhd->bhsm', q, km).astype(jnp.float32) * (X_HEAD_DIM ** -0.5)
    p = jax.nn.softmax(sc, axis=-1).astype(h.dtype)
    o = jnp.einsum('bhsm,bmhd->bshd', p, vm).reshape(B, S, D)
    return o @ w_xo


def peer(h, w_pq, sub_keys, u_tab, v_tab):
    B, S, D = h.shape
    T = B * S
    xt = h.reshape(T, D)
    q = (xt @ w_pq).reshape(T, PEER_HEADS, 2, PK_DIM // 2)
    s_half = jnp.einsum('thpc,hpkc->thpk', q, sub_keys).astype(jnp.float32)
    top_s, top_i = lax.top_k(s_half, PK_TOPK)
    cand_s = top_s[:, :, 0, :, None] + top_s[:, :, 1, None, :]
    cand_i = top_i[:, :, 0, :, None] * N_KEYS + top_i[:, :, 1, None, :]
    best_s, best_pos = lax.top_k(cand_s.reshape(T, PEER_HEADS, PK_TOPK * PK_TOPK), PK_TOPK)
    ids = jnp.take_along_axis(cand_i.reshape(T, PEER_HEADS, PK_TOPK * PK_TOPK), best_pos, axis=-1)
    gate = jax.nn.softmax(best_s, axis=-1)
    nblk = T // PEER_BLOCK

    def block(args):
        xb, idb, gb = args
        u = u_tab[idb]
        act = jax.nn.gelu(jnp.einsum('td,thkd->thk', xb, u).astype(jnp.float32), approximate=False)
        coef = (gb * act).astype(xb.dtype)
        return jnp.einsum('thk,thkd->td', coef, v_tab[idb])

    out = lax.map(block, (xt.reshape(nblk, PEER_BLOCK, D),
                          ids.reshape(nblk, PEER_BLOCK, PEER_HEADS, PK_TOPK),
                          gate.reshape(nblk, PEER_BLOCK, PEER_HEADS, PK_TOPK)))
    return out.reshape(B, S, D)


def setup_inputs(seed: int = 0) -> dict:
    key = jax.random.key(seed)
    ks = jax.random.split(key, 26)
    L = DEPTH

    def nrm(k, shape, scale):
        return jax.random.normal(k, shape, jnp.float32) * scale

    def gain(k, shape):
        return 1.0 + 0.02 * jax.random.normal(k, shape, jnp.float32)

    return {
        "x": nrm(ks[0], (BATCH, SEQ, D_MODEL), 1.0),
        "mem": nrm(ks[1], (BATCH, N_MEM, D_MODEL), 1.0),
        "g_mix": gain(ks[2], (L, D_MODEL)),
        "w_in": nrm(ks[3], (L, D_MODEL, IN_COLS), D_MODEL ** -0.5),
        "conv_w": nrm(ks[4], (L, CONV_W, M_WIDTH), CONV_W ** -0.5),
        "conv_b": nrm(ks[5], (L, M_WIDTH), 0.02),
        "w_mq": nrm(ks[6], (L, M_HEADS, M_HEAD_DIM, M_HEAD_DIM), M_HEAD_DIM ** -0.5),
        "w_mk": nrm(ks[7], (L, M_HEADS, M_HEAD_DIM, M_HEAD_DIM), M_HEAD_DIM ** -0.5),
        "b_mi": nrm(ks[8], (L, M_HEADS), 0.1),
        "b_mf": jnp.linspace(3.0, 6.0, M_HEADS, dtype=jnp.float32)[None, :] + nrm(ks[9], (L, M_HEADS), 0.1),
        "g_mhead": gain(ks[10], (L, M_HEADS, M_HEAD_DIM)),
        "g_ahead": gain(ks[11], (L, A_HEADS, A_HEAD_DIM)),
        "w_out": nrm(ks[12], (L, MIX_WIDTH, D_MODEL), MIX_WIDTH ** -0.5),
        "g_cross": gain(ks[13], (L, D_MODEL)),
        "g_mem": gain(ks[14], (L, D_MODEL)),
        "w_xq": nrm(ks[15], (L, D_MODEL, D_MODEL), D_MODEL ** -0.5),
        "w_xk": nrm(ks[16], (L, D_MODEL, D_MODEL), D_MODEL ** -0.5),
        "w_xv": nrm(ks[17], (L, D_MODEL, D_MODEL), D_MODEL ** -0.5),
        "w_xo": nrm(ks[18], (L, D_MODEL, D_MODEL), D_MODEL ** -0.5),
        "g_ffn": gain(ks[19], (L, D_MODEL)),
        "w_pq": nrm(ks[20], (L, D_MODEL, PEER_HEADS * PK_DIM), D_MODEL ** -0.5),
        "sub_keys": nrm(ks[21], (L, PEER_HEADS, 2, N_KEYS, PK_DIM // 2), (PK_DIM // 2) ** -0.5),
        "u_tab": nrm(ks[22], (L, N_EXPERTS, D_MODEL), D_MODEL ** -0.5),
        "v_tab": nrm(ks[23], (L, N_EXPERTS, D_MODEL), PEER_HEADS ** -0.5),
        "g_final": gain(ks[24], (D_MODEL,)),
    }


def reference(x, mem, g_mix, w_in, conv_w, conv_b, w_mq, w_mk, b_mi, b_mf, g_mhead, g_ahead, w_out,
              g_cross, g_mem, w_xq, w_xk, w_xv, w_xo, g_ffn, w_pq, sub_keys, u_tab, v_tab, g_final):
    B, S, _ = x.shape
    pos = jnp.arange(S)
    for l in range(DEPTH):
        h = rmsnorm(x, g_mix[l])
        proj = h @ w_in[l]
        m_in, m_v, m_o, m_i, m_f, a_q, a_k, a_v = jnp.split(proj, IN_SPLITS, axis=-1)
        c = jax.nn.silu(causal_conv(m_in, conv_w[l], conv_b[l])).reshape(B, S, M_HEADS, M_HEAD_DIM)
        mq = jnp.einsum('bshd,hde->bhse', c, w_mq[l])
        mk = jnp.einsum('bshd,hde->bhse', c, w_mk[l])
        mv = heads(m_v, M_HEADS)
        hm = mlstm_chunkwise(mq, mk, mv, (m_i + b_mi[l]).transpose(0, 2, 1), (m_f + b_mf[l]).transpose(0, 2, 1))
        hm = rmsnorm(hm.transpose(0, 2, 1, 3), g_mhead[l]).reshape(B, S, M_WIDTH) * jax.nn.sigmoid(m_o)
        qa = rope(heads(a_q, A_HEADS), pos)
        ka = rope(heads(a_k, A_HEADS), pos)
        va = heads(a_v, A_HEADS)
        ha = dilated_mixture(qa, ka, va)
        ha = rmsnorm(ha.transpose(0, 2, 1, 3), g_ahead[l]).reshape(B, S, A_WIDTH)
        x = x + jnp.concatenate([hm, ha], axis=-1) @ w_out[l]
        x = x + memory_cross_attention(rmsnorm(x, g_cross[l]), rmsnorm(mem, g_mem[l]),
                                       w_xq[l], w_xk[l], w_xv[l], w_xo[l])
        x = x + peer(rmsnorm(x, g_ffn[l]), w_pq[l], sub_keys[l], u_tab[l], v_tab[l])
    return rmsnorm(x, g_final)
```

```python
import functools
import math

import jax
import jax.numpy as jnp
from jax import lax
from jax.experimental import pallas as pl
from jax.experimental.pallas import tpu as pltpu

F32 = jnp.float32
BF16 = jnp.bfloat16

EPS = 1e-6
M_HEADS = 4
A_HEADS = 8
CONV_W = 4
DILATED = ((128, 1), (512, 4), (2048, 16))
DIL_BLK = 128
ROPE_THETA = 10000.0
X_HEADS = 4
PEER_HEADS = 8
N_KEYS = 128
PK_TOPK = 16
MLSTM_CHUNK = 256
VMEM_LIMIT = 56 * 1024 * 1024
NEG_INF = float("-inf")


def _cparams(sem):
    return pltpu.CompilerParams(dimension_semantics=sem, vmem_limit_bytes=VMEM_LIMIT)


def _norm_matmul_body(x_ref, g_ref, w_ref, o_ref, *rest, emit_h):
    if emit_h:
        h_out_ref, h_scr = rest
    else:
        (h_scr,) = rest

    @pl.when(pl.program_id(1) == 0)
    def _():
        x = x_ref[...]
        ms = jnp.mean(x * x, axis=-1, keepdims=True)
        h = ((x * lax.rsqrt(ms + EPS)) * g_ref[...]).astype(BF16)
        h_scr[...] = h
        if emit_h:
            h_out_ref[...] = h

    o_ref[...] = jnp.dot(h_scr[...], w_ref[...], preferred_element_type=F32).astype(o_ref.dtype)


def norm_matmul(x, g, w, *, tm, tn, out_dtype, emit_h=False, name):
    m, d = x.shape
    n = w.shape[1]
    out_shape = [jax.ShapeDtypeStruct((m, n), out_dtype)]
    out_specs = [pl.BlockSpec((tm, tn), lambda i, j: (i, j))]
    if emit_h:
        out_shape.append(jax.ShapeDtypeStruct((m, d), BF16))
        out_specs.append(pl.BlockSpec((tm, d), lambda i, j: (i, 0)))
    res = pl.pallas_call(
        functools.partial(_norm_matmul_body, emit_h=emit_h),
        grid=(m // tm, n // tn),
        in_specs=[pl.BlockSpec((tm, d), lambda i, j: (i, 0)),
                  pl.BlockSpec((1, d), lambda i, j: (0, 0)),
                  pl.BlockSpec((d, tn), lambda i, j: (0, j))],
        out_specs=out_specs,
        out_shape=out_shape,
        scratch_shapes=[pltpu.VMEM((tm, d), BF16)],
        compiler_params=_cparams(("parallel", "arbitrary")),
        name=name,
    )(x, g.reshape(1, d), w)
    return res if emit_h else res[0]


def _res_matmul_body(*refs, n_lhs):
    res_ref = refs[0]
    o_ref = refs[1 + 2 * n_lhs]
    acc = res_ref[...]
    for t in range(n_lhs):
        acc = acc + jnp.dot(refs[1 + 2 * t][...], refs[2 + 2 * t][...], preferred_element_type=F32)
    o_ref[...] = acc


def res_matmul(res, pairs, *, tm, tn, name):
    m, n = res.shape
    in_specs = [pl.BlockSpec((tm, tn), lambda i, j: (i, j))]
    args = [res]
    for a, w in pairs:
        k = a.shape[1]
        in_specs.append(pl.BlockSpec((tm, k), lambda i, j: (i, 0)))
        in_specs.append(pl.BlockSpec((k, tn), lambda i, j: (0, j)))
        args += [a, w]
    return pl.pallas_call(
        functools.partial(_res_matmul_body, n_lhs=len(pairs)),
        grid=(m // tm, n // tn),
        in_specs=in_specs,
        out_specs=pl.BlockSpec((tm, tn), lambda i, j: (i, j)),
        out_shape=jax.ShapeDtypeStruct((m, n), F32),
        compiler_params=_cparams(("parallel", "arbitrary")),
        name=name,
    )(*args)


def _mlstm_body(xin_ref, prev_ref, v_ref, og_ref, gates_ref, gbias_ref, cw_ref, cb_ref,
                wq_ref, wk_ref, gh_ref, o_ref, c_scr, n_scr, m_scr):
    L = MLSTM_CHUNK
    dh = xin_ref.shape[1]
    head = pl.program_id(1)
    chunk = pl.program_id(2)

    @pl.when(chunk == 0)
    def _():
        c_scr[...] = jnp.zeros_like(c_scr)
        n_scr[...] = jnp.zeros_like(n_scr)
        m_scr[...] = jnp.zeros_like(m_scr)

    xin = xin_ref[...]
    prev = jnp.where(chunk == 0, 0.0, prev_ref[...])
    xext = jnp.concatenate([prev, xin], axis=0)
    y = cb_ref[...] + jnp.zeros((L, dh), F32)
    for j in range(CONV_W):
        off = 8 - (CONV_W - 1) + j
        y = y + cw_ref[j:j + 1, :] * xext[off:off + L, :]
    cact = (y * jax.nn.sigmoid(y)).astype(BF16)
    q = jnp.dot(cact, wq_ref[...], preferred_element_type=F32)
    k = jnp.dot(cact, wk_ref[...], preferred_element_type=F32) * (dh ** -0.5)
    qb = q.astype(BF16)
    kb = k.astype(BF16)
    v = v_ref[...]
    vb = v.astype(BF16)

    gt = gates_ref[...] + gbias_ref[...]
    logf_all = jnp.minimum(gt, 0.0) - jnp.log(1.0 + jnp.exp(-jnp.abs(gt)))
    row = lax.broadcasted_iota(jnp.int32, (L, L), 0)
    col = lax.broadcasted_iota(jnp.int32, (L, L), 1)
    causal = col <= row
    tri = causal.astype(F32)
    cum_all = jnp.dot(tri, logf_all, preferred_element_type=F32, precision=lax.Precision.HIGHEST)
    lane = lax.broadcasted_iota(jnp.int32, (L, 128), 1)
    z = jnp.where(lane < M_HEADS, gt, cum_all)
    zt = z.T
    sub = lax.broadcasted_iota(jnp.int32, (128, L), 0)
    logi_col = jnp.sum(jnp.where(lane == head, z, 0.0), axis=1, keepdims=True)
    b_col = jnp.sum(jnp.where(lane == head + M_HEADS, z, 0.0), axis=1, keepdims=True)
    logi_row = jnp.sum(jnp.where(sub == head, zt, 0.0), axis=0, keepdims=True)
    b_row = jnp.sum(jnp.where(sub == head + M_HEADS, zt, 0.0), axis=0, keepdims=True)

    g_tot = b_col[L - 1:L, :]
    m_prev = m_scr[...]
    a_col = logi_col + g_tot - b_col
    m_new = jnp.maximum(g_tot + m_prev, jnp.max(a_col, axis=0, keepdims=True))
    decay = jnp.exp(g_tot + m_prev - m_new)
    w_col = jnp.exp(a_col - m_new)

    dmat = jnp.where(causal, b_col - b_row + logi_row, NEG_INF)
    m_inter = b_col + m_prev
    m_t = jnp.maximum(m_inter, jnp.max(dmat, axis=1, keepdims=True))
    s = lax.dot_general(qb, kb, (((1,), (1,)), ((), ())), preferred_element_type=F32) * jnp.exp(dmat - m_t)
    inter = jnp.exp(m_inter - m_t)
    c_prev = c_scr[...]
    n_prev = n_scr[...]
    num = (jnp.dot(s.astype(BF16), vb, preferred_element_type=F32)
           + inter * jnp.dot(qb, c_prev.astype(BF16), preferred_element_type=F32))
    den = jnp.sum(s, axis=1, keepdims=True) + inter * jnp.sum(q * n_prev, axis=1, keepdims=True)
    h = num / jnp.maximum(jnp.abs(den), jnp.exp(-m_t))

    c_scr[...] = decay * c_prev + lax.dot_general(
        kb, (w_col * v).astype(BF16), (((0,), (0,)), ((), ())), preferred_element_type=F32)
    n_scr[...] = decay * n_prev + jnp.sum(w_col * k, axis=0, keepdims=True)
    m_scr[...] = m_new

    ms = jnp.mean(h * h, axis=-1, keepdims=True)
    hn = (h * lax.rsqrt(ms + EPS)) * gh_ref[...]
    o_ref[...] = (hn * jax.nn.sigmoid(og_ref[...])).astype(o_ref.dtype)


def mlstm_group(proj, gates, gbias, conv_w, conv_b, wq, wk, g_mhead, *, batch, seq):
    L = MLSTM_CHUNK
    dh = wq.shape[-1]
    nc = seq // L
    rows8 = L // 8
    return pl.pallas_call(
        _mlstm_body,
        grid=(batch, M_HEADS, nc),
        in_specs=[
            pl.BlockSpec((L, dh), lambda b, h, c: (b * nc + c, h)),
            pl.BlockSpec((8, dh), lambda b, h, c: (jnp.maximum((b * nc + c) * rows8 - 1, 0), h)),
            pl.BlockSpec((L, dh), lambda b, h, c: (b * nc + c, M_HEADS + h)),
            pl.BlockSpec((L, dh), lambda b, h, c: (b * nc + c, 2 * M_HEADS + h)),
            pl.BlockSpec((L, 128), lambda b, h, c: (b * nc + c, 0)),
            pl.BlockSpec((1, 128), lambda b, h, c: (0, 0)),
            pl.BlockSpec((CONV_W, dh), lambda b, h, c: (0, h)),
            pl.BlockSpec((1, dh), lambda b, h, c: (0, h)),
            pl.BlockSpec((None, dh, dh), lambda b, h, c: (h, 0, 0)),
            pl.BlockSpec((None, dh, dh), lambda b, h, c: (h, 0, 0)),
            pl.BlockSpec((None, 1, dh), lambda b, h, c: (h, 0, 0)),
        ],
        out_specs=pl.BlockSpec((L, dh), lambda b, h, c: (b * nc + c, h)),
        out_shape=jax.ShapeDtypeStruct((batch * seq, M_HEADS * dh), BF16),
        scratch_shapes=[pltpu.VMEM((dh, dh), F32), pltpu.VMEM((1, dh), F32), pltpu.VMEM((1, 1), F32)],
        compiler_params=_cparams(("parallel", "parallel", "arbitrary")),
        name="mlstm_group",
    )(proj, proj, proj, proj, gates, gbias, conv_w, conv_b.reshape(1, -1), wq, wk,
      g_mhead.reshape(M_HEADS, 1, dh))


def _dilated_body(q_ref, k_ref, v_ref, cos_ref, sin_ref, gh_ref, o_ref,
                  q_scr, k_scr, acc_scr, m_scr, l_scr):
    S, dh = q_ref.shape
    blk = DIL_BLK
    half = dh // 2
    scale = dh ** -0.5
    CH = 512

    def rope_chunk(c, carry):
        r0 = pl.multiple_of(c * CH, CH)
        cs = cos_ref[pl.ds(r0, CH), :]
        sn = sin_ref[pl.ds(r0, CH), :]
        xq = q_ref[pl.ds(r0, CH), :]
        xk = k_ref[pl.ds(r0, CH), :]
        q_scr[pl.ds(r0, CH), :] = xq * cs + pltpu.roll(xq, half, 1) * sn
        k_scr[pl.ds(r0, CH), :] = xk * cs + pltpu.roll(xk, half, 1) * sn
        return carry

    lax.fori_loop(0, S // CH, rope_chunk, 0)

    qi = lax.broadcasted_iota(jnp.int32, (blk, blk), 0)
    kj = lax.broadcasted_iota(jnp.int32, (blk, blk), 1)
    mask_cur = kj <= qi
    mask_prev = kj >= qi

    for bi, (window, dil) in enumerate(DILATED):
        nb = (S // dil) // blk

        def unit(u, carry, dil=dil, nb=nb, first=(bi == 0)):
            r = u // nb
            n = u % nb
            q0 = r + n * (blk * dil)
            p0 = jnp.where(n > 0, q0 - blk * dil, q0)
            rows_q = pl.ds(q0, blk, stride=dil) if dil > 1 else pl.ds(q0, blk)
            rows_p = pl.ds(p0, blk, stride=dil) if dil > 1 else pl.ds(p0, blk)
            qu = q_scr[rows_q, :].astype(BF16)
            kc = k_scr[rows_q, :].astype(BF16)
            kp = k_scr[rows_p, :].astype(BF16)
            vc = v_ref[rows_q, :].astype(BF16)
            vp = v_ref[rows_p, :].astype(BF16)
            nt = (((1,), (1,)), ((), ()))
            sc = lax.dot_general(qu, kc, nt, preferred_element_type=F32) * scale
            sp = lax.dot_general(qu, kp, nt, preferred_element_type=F32) * scale
            sc = jnp.where(mask_cur, sc, NEG_INF)
            sp = jnp.where(jnp.logical_and(mask_prev, n > 0), sp, NEG_INF)
            m_u = jnp.maximum(jnp.max(sc, axis=1, keepdims=True), jnp.max(sp, axis=1, keepdims=True))
            pc = jnp.exp(sc - m_u)
            pp = jnp.exp(sp - m_u)
            s_u = jnp.sum(pc, axis=1, keepdims=True) + jnp.sum(pp, axis=1, keepdims=True)
            pv = (jnp.dot(pc.astype(BF16), vc, preferred_element_type=F32)
                  + jnp.dot(pp.astype(BF16), vp, preferred_element_type=F32))
            if first:
                acc_scr[rows_q, :] = pv
                m_scr[rows_q, :] = jnp.broadcast_to(m_u, (blk, dh))
                l_scr[rows_q, :] = jnp.broadcast_to(s_u, (blk, dh))
            else:
                m_old = m_scr[rows_q, :]
                m_new = jnp.maximum(m_old, m_u)
                alpha = jnp.exp(m_old - m_new)
                beta = jnp.exp(m_u - m_new)
                acc_scr[rows_q, :] = alpha * acc_scr[rows_q, :] + beta * pv
                l_scr[rows_q, :] = alpha * l_scr[rows_q, :] + beta * s_u
                m_scr[rows_q, :] = m_new
            return carry

        lax.fori_loop(0, dil * nb, unit, 0)

    def finish(c, carry):
        r0 = pl.multiple_of(c * CH, CH)
        out = acc_scr[pl.ds(r0, CH), :] / l_scr[pl.ds(r0, CH), :]
        ms = jnp.mean(out * out, axis=-1, keepdims=True)
        o_ref[pl.ds(r0, CH), :] = ((out * lax.rsqrt(ms + EPS)) * gh_ref[...]).astype(o_ref.dtype)
        return carry

    lax.fori_loop(0, S // CH, finish, 0)


def dilated_group(proj, cos_t, sin_t, g_ahead, *, batch, seq, col0):
    dh = g_ahead.shape[-1]
    qoff = col0 // dh
    return pl.pallas_call(
        _dilated_body,
        grid=(batch, A_HEADS),
        in_specs=[
            pl.BlockSpec((seq, dh), lambda b, h: (b, qoff + h)),
            pl.BlockSpec((seq, dh), lambda b, h: (b, qoff + A_HEADS + h)),
            pl.BlockSpec((seq, dh), lambda b, h: (b, qoff + 2 * A_HEADS + h)),
            pl.BlockSpec((seq, dh), lambda b, h: (0, 0)),
            pl.BlockSpec((seq, dh), lambda b, h: (0, 0)),
            pl.BlockSpec((None, 1, dh), lambda b, h: (h, 0, 0)),
        ],
        out_specs=pl.BlockSpec((seq, dh), lambda b, h: (b, h)),
        out_shape=jax.ShapeDtypeStruct((batch * seq, A_HEADS * dh), BF16),
        scratch_shapes=[pltpu.VMEM((seq, dh), F32)] * 5,
        compiler_params=_cparams(("parallel", "parallel")),
        name="dilated_group",
    )(proj, proj, proj, cos_t, sin_t, g_ahead.reshape(A_HEADS, 1, dh))


def _cross_attn_body(q_ref, k_ref, v_ref, o_ref):
    d = q_ref.shape[1]
    dh = d // X_HEADS
    scale = dh ** -0.5
    for hh in range(X_HEADS):
        sl = slice(hh * dh, (hh + 1) * dh)
        sc = lax.dot_general(q_ref[:, sl], k_ref[:, sl], (((1,), (1,)), ((), ())),
                             preferred_element_type=F32) * scale
        e = jnp.exp(sc - jnp.max(sc, axis=-1, keepdims=True))
        p = (e / jnp.sum(e, axis=-1, keepdims=True)).astype(BF16)
        o_ref[:, sl] = jnp.dot(p, v_ref[:, sl], preferred_element_type=F32).astype(o_ref.dtype)


def cross_attention(q, kv, *, batch, seq, n_mem, tm):
    d = q.shape[1]
    nt = seq // tm
    return pl.pallas_call(
        _cross_attn_body,
        grid=(batch, nt),
        in_specs=[pl.BlockSpec((tm, d), lambda b, i: (b * nt + i, 0)),
                  pl.BlockSpec((n_mem, d), lambda b, i: (b, 0)),
                  pl.BlockSpec((n_mem, d), lambda b, i: (b, 1))],
        out_specs=pl.BlockSpec((tm, d), lambda b, i: (b * nt + i, 0)),
        out_shape=jax.ShapeDtypeStruct((batch * seq, d), BF16),
        compiler_params=_cparams(("parallel", "arbitrary")),
        name="cross_attention",
    )(q, kv, kv)


def _take_top(vals, count):
    rows = vals.shape[0]
    ridx = lax.broadcasted_iota(jnp.int32, vals.shape, 0)
    tops = []
    for _ in range(count):
        mx = jnp.max(vals, axis=0, keepdims=True)
        first = jnp.min(jnp.where(vals == mx, ridx, rows), axis=0, keepdims=True)
        vals = jnp.where(ridx == first, NEG_INF, vals)
        tops.append(mx)
    return tops


def _peer_route_body(q_ref, keys_ref, s1_ref, s2_ref, e1_ref, e2_ref, tau_ref):
    tt = q_ref.shape[0]
    nt = (((1,), (1,)), ((), ()))
    taus = []
    for h in range(PEER_HEADS):
        s = []
        tops = []
        for p in range(2):
            c0 = (2 * h + p) * N_KEYS
            sp = lax.dot_general(keys_ref[h, p], q_ref[:, c0:c0 + N_KEYS], nt,
                                 preferred_element_type=F32)
            s.append(sp)
            tops.append(_take_top(sp, PK_TOPK))
        t2 = jnp.concatenate(tops[1], axis=0)
        cand = jnp.concatenate([tops[0][a] + t2 for a in range(PK_TOPK)], axis=0)
        best = _take_top(cand, PK_TOPK)
        m_h = best[0]
        zsum = jnp.zeros((1, tt), F32)
        for b in best:
            zsum = zsum + jnp.exp(b - m_h)
        s1_ref[h] = s[0]
        s2_ref[h] = s[1]
        e1_ref[h] = jnp.exp(s[0] - tops[0][0]) / zsum
        e2_ref[h] = jnp.exp(s[1] - tops[1][0])
        taus.append(best[PK_TOPK - 1])
    tau_ref[...] = jnp.concatenate(taus, axis=0)


def peer_route(q, keys, *, tt):
    t, d = q.shape
    big = jax.ShapeDtypeStruct((PEER_HEADS, N_KEYS, t), F32)
    big_spec = pl.BlockSpec((PEER_HEADS, N_KEYS, tt), lambda i: (0, 0, i))
    return pl.pallas_call(
        _peer_route_body,
        grid=(t // tt,),
        in_specs=[pl.BlockSpec((tt, d), lambda i: (i, 0)),
                  pl.BlockSpec(keys.shape, lambda i: (0, 0, 0, 0))],
        out_specs=[big_spec, big_spec, big_spec, big_spec,
                   pl.BlockSpec((PEER_HEADS, tt), lambda i: (0, i))],
        out_shape=[big, big, big, big, jax.ShapeDtypeStruct((PEER_HEADS, t), F32)],
        compiler_params=_cparams(("parallel",)),
        name="peer_route",
    )(q, keys)


def _peer_dense_body(h_ref, u_ref, vt_ref, s1_ref, e1_ref, s2_ref, e2_ref, tau_ref, o_ref,
                     acc_scr, coef_scr):
    e_step = pl.program_id(1)
    eb = u_ref.shape[0]
    n_i = eb // N_KEYS

    @pl.when(e_step == 0)
    def _():
        acc_scr[...] = jnp.zeros_like(acc_scr)

    hf = h_ref[...]
    for ii in range(n_i):
        rows = slice(ii * N_KEYS, (ii + 1) * N_KEYS)
        act = lax.dot_general(u_ref[rows, :], hf, (((1,), (1,)), ((), ())),
                              preferred_element_type=F32)
        gel = 0.5 * act * (1.0 + lax.erf(act * (2.0 ** -0.5)))
        gate = jnp.zeros_like(act)
        for h in range(PEER_HEADS):
            tot = s2_ref[h] + s1_ref[h, ii:ii + 1, :]
            gate = gate + jnp.where(tot >= tau_ref[h:h + 1, :], e2_ref[h] * e1_ref[h, ii:ii + 1, :], 0.0)
        coef_scr[rows, :] = (gate * gel).astype(BF16)
    acc_scr[...] += jnp.dot(vt_ref[...], coef_scr[...], preferred_element_type=F32)

    @pl.when(e_step == pl.num_programs(1) - 1)
    def _():
        o_ref[...] = acc_scr[...].T


def peer_dense(hf, u, vt, s1, e1, s2, e2, tau, *, tt, eb):
    t, d = hf.shape
    ne = u.shape[0]
    n_i = eb // N_KEYS
    return pl.pallas_call(
        _peer_dense_body,
        grid=(t // tt, ne // eb),
        in_specs=[
            pl.BlockSpec((tt, d), lambda i, e: (i, 0)),
            pl.BlockSpec((eb, d), lambda i, e: (e, 0)),
            pl.BlockSpec((d, eb), lambda i, e: (0, e)),
            pl.BlockSpec((PEER_HEADS, n_i, tt), lambda i, e: (0, e, i)),
            pl.BlockSpec((PEER_HEADS, n_i, tt), lambda i, e: (0, e, i)),
            pl.BlockSpec((PEER_HEADS, N_KEYS, tt), lambda i, e: (0, 0, i)),
            pl.BlockSpec((PEER_HEADS, N_KEYS, tt), lambda i, e: (0, 0, i)),
            pl.BlockSpec((PEER_HEADS, tt), lambda i, e: (0, i)),
        ],
        out_specs=pl.BlockSpec((tt, d), lambda i, e: (i, 0)),
        out_shape=jax.ShapeDtypeStruct((t, d), F32),
        scratch_shapes=[pltpu.VMEM((d, tt), F32), pltpu.VMEM((eb, tt), BF16)],
        compiler_params=_cparams(("parallel", "arbitrary")),
        name="peer_dense",
    )(hf, u, vt, s1, e1, s2, e2, tau)


def _add_norm_body(x_ref, y_ref, g_ref, o_ref):
    x = x_ref[...] + y_ref[...]
    ms = jnp.mean(x * x, axis=-1, keepdims=True)
    o_ref[...] = (x * lax.rsqrt(ms + EPS)) * g_ref[...]


def add_norm(x, y, g, *, tm):
    m, d = x.shape
    return pl.pallas_call(
        _add_norm_body,
        grid=(m // tm,),
        in_specs=[pl.BlockSpec((tm, d), lambda i: (i, 0)),
                  pl.BlockSpec((tm, d), lambda i: (i, 0)),
                  pl.BlockSpec((1, d), lambda i: (0, 0))],
        out_specs=pl.BlockSpec((tm, d), lambda i: (i, 0)),
        out_shape=jax.ShapeDtypeStruct((m, d), F32),
        compiler_params=_cparams(("parallel",)),
        name="add_norm",
    )(x, y, g.reshape(1, d))


def _rope_tables(seq, dh):
    half = dh // 2
    inv = ROPE_THETA ** (-jnp.arange(half, dtype=F32) / half)
    ang = jnp.arange(seq, dtype=F32)[:, None] * inv[None, :]
    cos, sin = jnp.cos(ang), jnp.sin(ang)
    return jnp.concatenate([cos, cos], axis=-1), jnp.concatenate([-sin, sin], axis=-1)


def kernel(x, mem, g_mix, w_in, conv_w, conv_b, w_mq, w_mk, b_mi, b_mf, g_mhead, g_ahead, w_out, g_cross,
           g_mem, w_xq, w_xk, w_xv, w_xo, g_ffn, w_pq, sub_keys, u_tab, v_tab, g_final):
    batch, seq, d = x.shape
    n_mem = mem.shape[1]
    depth = w_in.shape[0]
    assert depth == 1, "the final norm is fused after the single layer"
    m_width = conv_w.shape[-1]
    a_dh = g_ahead.shape[-1]
    a_width = A_HEADS * a_dh
    t = batch * seq
    xt = x.reshape(t, d)
    memt = mem.reshape(batch * n_mem, d)
    cos_t, sin_t = _rope_tables(seq, a_dh)

    for l in range(depth):
        gate0 = 3 * m_width
        gate1 = gate0 + 2 * M_HEADS
        w_main = jnp.concatenate([w_in[l][:, :gate0], w_in[l][:, gate1:]], axis=1).astype(BF16)
        w_gate = jnp.pad(w_in[l][:, gate0:gate1], ((0, 0), (0, 128 - 2 * M_HEADS))).astype(BF16)
        proj = norm_matmul(xt, g_mix[l], w_main, tm=512, tn=512, out_dtype=F32, name="in_proj")
        gates = norm_matmul(xt, g_mix[l], w_gate, tm=512, tn=128, out_dtype=F32, name="gate_proj")
        gbias = jnp.pad(jnp.concatenate([b_mi[l], b_mf[l]]), (0, 128 - 2 * M_HEADS)).reshape(1, 128)

        hm = mlstm_group(proj, gates, gbias, conv_w[l], conv_b[l], w_mq[l].astype(BF16), w_mk[l].astype(BF16),
                         g_mhead[l], batch=batch, seq=seq)
        ha = dilated_group(proj, cos_t, sin_t, g_ahead[l], batch=batch, seq=seq, col0=3 * m_width)
        w_o = w_out[l].astype(BF16)
        x1 = res_matmul(xt, [(hm, w_o[:m_width]), (ha, w_o[m_width:])], tm=512, tn=512, name="out_proj")

        q_x = norm_matmul(x1, g_cross[l], w_xq[l].astype(BF16), tm=512, tn=512, out_dtype=BF16, name="xq_proj")
        w_kv = jnp.concatenate([w_xk[l], w_xv[l]], axis=1).astype(BF16)
        kv = norm_matmul(memt, g_mem[l], w_kv, tm=n_mem, tn=512, out_dtype=BF16, name="xkv_proj")
        o_x = cross_attention(q_x, kv, batch=batch, seq=seq, n_mem=n_mem, tm=512)
        x2 = res_matmul(x1, [(o_x, w_xo[l].astype(BF16))], tm=512, tn=512, name="xo_proj")

        q_p, h_ffn = norm_matmul(x2, g_ffn[l], w_pq[l].astype(BF16), tm=512, tn=512, out_dtype=BF16,
                                 emit_h=True, name="pq_proj")
        s1, s2, e1, e2, tau = peer_route(q_p, sub_keys[l].astype(BF16), tt=256)
        y = peer_dense(h_ffn, u_tab[l].astype(BF16), v_tab[l].T.astype(BF16), s1, e1, s2, e2, tau,
                       tt=512, eb=1024)
        out = add_norm(x2, y, g_final, tm=256)
    return out.reshape(batch, seq, d)
```
